```python
import jax, jax.numpy as jnp
from jax import lax
import numpy as np

D_MODEL = 1024
BATCH = 8
SEQ = 4096
DEPTH = 2

MIX_WIDTH = D_MODEL
LRU_WIDTH = MIX_WIDTH // 2
LRU_HEADS = 8
LRU_HEAD_DIM = LRU_WIDTH // LRU_HEADS
LRU_CONV_WIDTH = 4
LRU_C = 8.0
SB_HEADS = 8
SB_HEAD_DIM = (MIX_WIDTH - LRU_WIDTH) // SB_HEADS
SB_WIDTH = SB_HEADS * SB_HEAD_DIM
Q_BLOCK = 128
IN_PROJ_WIDTH = 2 * LRU_WIDTH + 3 * SB_WIDTH
CONF_WIDTH = D_MODEL
CONF_CONV_WIDTH = 31
D_FF = 4 * D_MODEL
RMS_EPS = 1e-6
LN_EPS = 1e-5
N_EVEN = (DEPTH + 1) // 2
N_ODD = DEPTH // 2

kernel_name = "hybrid_rglru_stickbreak_conformer_trunk"


def rms_norm(x, g):
    xf = x.astype(jnp.float32)
    y = xf * lax.rsqrt(jnp.mean(xf * xf, axis=-1, keepdims=True) + RMS_EPS)
    return (y * g.astype(jnp.float32)).astype(x.dtype)


def layer_norm(x, g, b):
    xf = x.astype(jnp.float32)
    mu = jnp.mean(xf, axis=-1, keepdims=True)
    xc = xf - mu
    var = jnp.mean(xc * xc, axis=-1, keepdims=True)
    y = xc * lax.rsqrt(var + LN_EPS) * g.astype(jnp.float32) + b.astype(jnp.float32)
    return y.astype(x.dtype)


def causal_depthwise_conv(x, w, b):
    K, C = w.shape
    y = lax.conv_general_dilated(
        x, w[:, None, :].astype(x.dtype), window_strides=(1,), padding=[(K - 1, 0)],
        dimension_numbers=('NWC', 'WIO', 'NWC'), feature_group_count=C)
    return y + b


def rg_lru(x, gate_a_w, gate_a_b, gate_x_w, gate_x_b, lam):
    B, S, _ = x.shape
    xh = x.reshape(B, S, LRU_HEADS, LRU_HEAD_DIM)
    r = jax.nn.sigmoid(jnp.einsum('bshi,hij->bshj', xh, gate_a_w).reshape(B, S, LRU_WIDTH) + gate_a_b).astype(jnp.float32)
    i = jax.nn.sigmoid(jnp.einsum('bshi,hij->bshj', xh, gate_x_w).reshape(B, S, LRU_WIDTH) + gate_x_b).astype(jnp.float32)
    log_a = LRU_C * r * jax.nn.log_sigmoid(lam.astype(jnp.float32))
    a = jnp.exp(log_a)
    inp = jnp.sqrt(jnp.maximum(-jnp.expm1(2.0 * log_a), 0.0)) * (i * x.astype(jnp.float32))

    def combine(left, right):
        a_l, b_l = left
        a_r, b_r = right
        return a_l * a_r, a_r * b_l + b_r

    _, h = lax.associative_scan(combine, (a, inp), axis=1)
    return h.astype(x.dtype)


def stick_breaking_attention(q, k, v):
    B, S, H, Dh = q.shape
    scale = Dh ** -0.5
    out_blocks = []
    for blk in range(S // Q_BLOCK):
        q0 = blk * Q_BLOCK
        q1 = q0 + Q_BLOCK
        z = jnp.einsum('bthd,bshd->bhts', q[:, q0:q1].astype(jnp.float32),
                       k[:, :q1].astype(jnp.float32)) * scale
        t_pos = jnp.arange(q0, q1)[:, None]
        s_pos = jnp.arange(q1)[None, :]
        causal = s_pos < t_pos
        log_keep = jnp.where(causal, jax.nn.log_sigmoid(-z), 0.0)
        log_remain = lax.cumsum(log_keep, axis=3, reverse=True) - log_keep
        w = jnp.where(causal, jnp.exp(jax.nn.log_sigmoid(z) + log_remain), 0.0)
        out_blocks.append(jnp.einsum('bhts,bshd->bthd', w, v[:, :q1].astype(jnp.float32)))
    return jnp.concatenate(out_blocks, axis=1).astype(q.dtype)


def lru_sb_mixer(h, w_in, conv_w, conv_b, ga_w, ga_b, gx_w, gx_b, lam, w_out):
    B, S, _ = h.shape
    proj = h @ w_in
    gate, x_rec, q, k, v = jnp.split(
        proj, [LRU_WIDTH, 2 * LRU_WIDTH, 2 * LRU_WIDTH + SB_WIDTH, 2 * LRU_WIDTH + 2 * SB_WIDTH], axis=-1)
    x_rec = causal_depthwise_conv(x_rec, conv_w, conv_b)
    y_lru = jax.nn.gelu(gate) * rg_lru(x_rec, ga_w, ga_b, gx_w, gx_b, lam)
    y_sb = stick_breaking_attention(
        q.reshape(B, S, SB_HEADS, SB_HEAD_DIM),
        k.reshape(B, S, SB_HEADS, SB_HEAD_DIM),
        v.reshape(B, S, SB_HEADS, SB_HEAD_DIM)).reshape(B, S, SB_WIDTH)
    return jnp.concatenate([y_lru, y_sb], axis=-1) @ w_out


def conformer_conv_module(h, pw1_w, pw1_b, dw_w, dw_b, ln_g, ln_b, pw2_w, pw2_b):
    u = h @ pw1_w + pw1_b
    u = u[..., :CONF_WIDTH] * jax.nn.sigmoid(u[..., CONF_WIDTH:])
    u = causal_depthwise_conv(u, dw_w, dw_b)
    u = jax.nn.silu(layer_norm(u, ln_g, ln_b))
    return u @ pw2_w + pw2_b


def sq_relu_mlp(h, w1, w2):
    a = jax.nn.relu(h @ w1)
    return (a * a) @ w2


def setup_inputs(seed: int = 0) -> dict:
    key = jax.random.key(seed)
    ks = iter(jax.random.split(key, 40))
    f32 = jnp.float32

    def nrm(shape, fan_in):
        return jax.random.normal(next(ks), shape, f32) * (fan_in ** -0.5)

    def gain(shape):
        return 1.0 + 0.02 * jax.random.normal(next(ks), shape, f32)

    def bias(shape):
        return 0.02 * jax.random.normal(next(ks), shape, f32)

    x = jax.random.normal(next(ks), (BATCH, SEQ, D_MODEL), f32)
    ev_norm_g = gain((N_EVEN, D_MODEL))
    ev_w_in = nrm((N_EVEN, D_MODEL, IN_PROJ_WIDTH), D_MODEL)
    ev_conv_w = nrm((N_EVEN, LRU_CONV_WIDTH, LRU_WIDTH), LRU_CONV_WIDTH)
    ev_conv_b = bias((N_EVEN, LRU_WIDTH))
    ev_gate_a_w = nrm((N_EVEN, LRU_HEADS, LRU_HEAD_DIM, LRU_HEAD_DIM), LRU_HEAD_DIM)
    ev_gate_a_b = bias((N_EVEN, LRU_WIDTH))
    ev_gate_x_w = nrm((N_EVEN, LRU_HEADS, LRU_HEAD_DIM, LRU_HEAD_DIM), LRU_HEAD_DIM)
    ev_gate_x_b = bias((N_EVEN, LRU_WIDTH))
    u = jax.random.uniform(next(ks), (N_EVEN, LRU_WIDTH), f32, 0.9, 0.999)
    a0 = u ** (1.0 / LRU_C)
    ev_lam = jnp.log(a0) - jnp.log1p(-a0)
    ev_w_out = nrm((N_EVEN, MIX_WIDTH, D_MODEL), MIX_WIDTH)
    od_norm_g = gain((N_ODD, D_MODEL))
    od_pw1_w = nrm((N_ODD, D_MODEL, 2 * CONF_WIDTH), D_MODEL)
    od_pw1_b = bias((N_ODD, 2 * CONF_WIDTH))
    od_dw_w = nrm((N_ODD, CONF_CONV_WIDTH, CONF_WIDTH), CONF_CONV_WIDTH)
    od_dw_b = bias((N_ODD, CONF_WIDTH))
    od_ln_g = gain((N_ODD, CONF_WIDTH))
    od_ln_b = bias((N_ODD, CONF_WIDTH))
    od_pw2_w = nrm((N_ODD, CONF_WIDTH, D_MODEL), CONF_WIDTH)
    od_pw2_b = bias((N_ODD, D_MODEL))
    mlp_norm_g = gain((DEPTH, D_MODEL))
    mlp_w1 = nrm((DEPTH, D_MODEL, D_FF), D_MODEL)
    mlp_w2 = nrm((DEPTH, D_FF, D_MODEL), D_FF)
    final_g = gain((D_MODEL,))
    return {
        "x": x,
        "ev_norm_g": ev_norm_g, "ev_w_in": ev_w_in, "ev_conv_w": ev_conv_w, "ev_conv_b": ev_conv_b,
        "ev_gate_a_w": ev_gate_a_w, "ev_gate_a_b": ev_gate_a_b, "ev_gate_x_w": ev_gate_x_w,
        "ev_gate_x_b": ev_gate_x_b, "ev_lam": ev_lam, "ev_w_out": ev_w_out,
        "od_norm_g": od_norm_g, "od_pw1_w": od_pw1_w, "od_pw1_b": od_pw1_b, "od_dw_w": od_dw_w,
        "od_dw_b": od_dw_b, "od_ln_g": od_ln_g, "od_ln_b": od_ln_b, "od_pw2_w": od_pw2_w,
        "od_pw2_b": od_pw2_b,
        "mlp_norm_g": mlp_norm_g, "mlp_w1": mlp_w1, "mlp_w2": mlp_w2, "final_g": final_g,
    }


def reference(x, ev_norm_g, ev_w_in, ev_conv_w, ev_conv_b, ev_gate_a_w, ev_gate_a_b,
              ev_gate_x_w, ev_gate_x_b, ev_lam, ev_w_out,
              od_norm_g, od_pw1_w, od_pw1_b, od_dw_w, od_dw_b, od_ln_g, od_ln_b,
              od_pw2_w, od_pw2_b, mlp_norm_g, mlp_w1, mlp_w2, final_g):
    for layer in range(DEPTH):
        j = layer // 2
        if layer % 2 == 0:
            h = rms_norm(x, ev_norm_g[j])
            x = x + lru_sb_mixer(h, ev_w_in[j], ev_conv_w[j], ev_conv_b[j], ev_gate_a_w[j],
                                 ev_gate_a_b[j], ev_gate_x_w[j], ev_gate_x_b[j], ev_lam[j],
                                 ev_w_out[j])
        else:
            h = rms_norm(x, od_norm_g[j])
            x = x + conformer_conv_module(h, od_pw1_w[j], od_pw1_b[j], od_dw_w[j], od_dw_b[j],
                                          od_ln_g[j], od_ln_b[j], od_pw2_w[j], od_pw2_b[j])
        h = rms_norm(x, mlp_norm_g[layer])
        x = x + sq_relu_mlp(h, mlp_w1[layer], mlp_w2[layer])
    return rms_norm(x, final_g)
```

```python
import functools

import jax
import jax.numpy as jnp
from jax import lax
from jax.experimental import pallas as pl
from jax.experimental.pallas import tpu as pltpu

F32 = jnp.float32
BF16 = jnp.bfloat16

RMS_EPS = 1e-6
LN_EPS = 1e-5
LRU_C = 8.0
LRU_HEADS = 8
SB_HEAD_DIM = 64

LANES = 128
SUBLANES = 8
VMEM_LIMIT = 56 * 1024 * 1024


def _params(sem):
    return pltpu.CompilerParams(dimension_semantics=sem, vmem_limit_bytes=VMEM_LIMIT)


def _const_spec(shape):
    nd = len(shape)
    return pl.BlockSpec(shape, lambda *_: (0,) * nd)


def _rms(x, g):
    ms = jnp.mean(x * x, axis=-1, keepdims=True)
    return x * lax.rsqrt(ms + RMS_EPS) * g


def _dot(a, b):
    return jnp.dot(a, b, preferred_element_type=F32)


def _inproj_kernel(x_ref, g_ref, w_ref, gate_ref, xrec_ref, q_ref, k_ref, v_ref, *, lru_w, sb_w, scale):
    h = _rms(x_ref[...], g_ref[...]).astype(BF16)
    o = 0
    gate_ref[...] = _dot(h, w_ref[:, o:o + lru_w]); o += lru_w
    xrec_ref[...] = _dot(h, w_ref[:, o:o + lru_w]); o += lru_w
    q_ref[...] = (_dot(h, w_ref[:, o:o + sb_w]) * scale).astype(BF16); o += sb_w
    k_ref[...] = _dot(h, w_ref[:, o:o + sb_w]).astype(BF16); o += sb_w
    v_ref[...] = _dot(h, w_ref[:, o:o + sb_w]).astype(BF16)


def _inproj(x, g, w, lru_w, sb_w, tm):
    T, D = x.shape
    row = lambda w_: pl.BlockSpec((tm, w_), lambda i: (i, 0))
    return pl.pallas_call(
        functools.partial(_inproj_kernel, lru_w=lru_w, sb_w=sb_w, scale=SB_HEAD_DIM ** -0.5),
        grid=(T // tm,),
        in_specs=[row(D), _const_spec(g.shape), _const_spec(w.shape)],
        out_specs=[row(lru_w), row(lru_w), row(sb_w), row(sb_w), row(sb_w)],
        out_shape=[jax.ShapeDtypeStruct((T, lru_w), F32), jax.ShapeDtypeStruct((T, lru_w), F32),
                   jax.ShapeDtypeStruct((T, sb_w), BF16), jax.ShapeDtypeStruct((T, sb_w), BF16),
                   jax.ShapeDtypeStruct((T, sb_w), BF16)],
        compiler_params=_params(("parallel",)),
        name="inproj",
    )(x, g, w)


def _lru_kernel(xrec_ref, gate_ref, cw_ref, cb_ref, wg_ref, bg_ref, c_ref, o_ref,
                xpad, a_s, b_s, h_s, hc_s, *, lt, cw, width):
    i = pl.program_id(1)

    @pl.when(i == 0)
    def _():
        xpad[0:SUBLANES, :] = jnp.zeros((SUBLANES, width), F32)
        hc_s[...] = jnp.zeros((SUBLANES, width), F32)

    xpad[SUBLANES:SUBLANES + lt, :] = xrec_ref[...]
    xc = cb_ref[...] + cw_ref[0:1, :] * xpad[pl.ds(SUBLANES - (cw - 1), lt), :]
    for k in range(1, cw):
        xc = xc + cw_ref[k:k + 1, :] * xpad[pl.ds(SUBLANES - (cw - 1) + k, lt), :]
    xpad[0:SUBLANES, :] = xrec_ref[lt - SUBLANES:lt, :]

    g2 = _dot(xc.astype(BF16), wg_ref[...]) + bg_ref[...]
    r = jax.nn.sigmoid(g2[:, :width])
    ig = jax.nn.sigmoid(g2[:, width:])
    a = jnp.exp(r * c_ref[...])
    a_s[...] = a
    b_s[...] = jnp.sqrt(jnp.maximum(1.0 - a * a, 0.0)) * (ig * xc)

    rowi = lax.broadcasted_iota(jnp.int32, (SUBLANES, width), 0)

    def group(gi, hc):
        st = pl.multiple_of(gi * SUBLANES, SUBLANES)
        A = a_s[pl.ds(st, SUBLANES), :]
        Bv = b_s[pl.ds(st, SUBLANES), :]
        for sh in (1, 2, 4):
            ok = rowi >= sh
            A_sh = jnp.where(ok, pltpu.roll(A, sh, axis=0), 1.0)
            B_sh = jnp.where(ok, pltpu.roll(Bv, sh, axis=0), 0.0)
            Bv = A * B_sh + Bv
            A = A * A_sh
        h = A * hc + Bv
        h_s[pl.ds(st, SUBLANES), :] = h
        return jnp.broadcast_to(h[SUBLANES - 1:SUBLANES, :], (SUBLANES, width))

    hc_s[...] = lax.fori_loop(0, lt // SUBLANES, group, hc_s[...], unroll=4)
    o_ref[...] = (jax.nn.gelu(gate_ref[...]) * h_s[...]).astype(BF16)


def _lru(xrec, gate, cw, cb, wg, bg, c, B, S, lt):
    T, width = xrec.shape
    ns = S // lt
    row = pl.BlockSpec((lt, width), lambda b, i: (b * ns + i, 0))
    return pl.pallas_call(
        functools.partial(_lru_kernel, lt=lt, cw=cw.shape[0], width=width),
        grid=(B, ns),
        in_specs=[row, row, _const_spec(cw.shape), _const_spec(cb.shape), _const_spec(wg.shape),
                  _const_spec(bg.shape), _const_spec(c.shape)],
        out_specs=row,
        out_shape=jax.ShapeDtypeStruct((T, width), BF16),
        scratch_shapes=[pltpu.VMEM((lt + SUBLANES, width), F32), pltpu.VMEM((lt, width), F32),
                        pltpu.VMEM((lt, width), F32), pltpu.VMEM((lt, width), F32),
                        pltpu.VMEM((SUBLANES, width), F32)],
        compiler_params=_params(("arbitrary", "arbitrary")),
        name="lru",
    )(xrec, gate, cw, cb, wg, bg, c)


def _sb_kernel(q_ref, k_ref, v_ref, o_ref, *, tq, dh):
    i = pl.program_id(2)
    nh = LANES // dh
    q = q_ref[...]
    lane = lax.broadcasted_iota(jnp.int32, (1, LANES), 1)
    qh = [jnp.where(lane // dh == h, q, jnp.zeros_like(q)) for h in range(nh)]
    row = lax.broadcasted_iota(jnp.int32, (tq, tq), 0)
    col = lax.broadcasted_iota(jnp.int32, (tq, tq), 1)
    tri = (row >= col).astype(BF16)
    causal = col < row
    rep = tq // LANES

    def tile(j, carry, diag):
        st = pl.multiple_of(j * tq, tq)
        ks = k_ref[pl.ds(st, tq), :]
        vs = v_ref[pl.ds(st, tq), :]
        out = []
        for h in range(nh):
            c, acc = carry[h]
            z = lax.dot_general(qh[h], ks, (((1,), (1,)), ((), ())), preferred_element_type=F32)
            lk = jnp.minimum(-z, 0.0) - jnp.log(1.0 + jnp.exp(-jnp.abs(z)))
            if diag:
                lk = jnp.where(causal, lk, 0.0)
            hi = lk.astype(BF16)
            lo = (lk - hi.astype(F32)).astype(BF16)
            cum = _dot(hi, tri) + _dot(lo, tri)
            w = jnp.exp(z + cum + pltpu.repeat(c, rep, 1))
            if diag:
                w = jnp.where(causal, w, 0.0)
            acc = acc + _dot(w.astype(BF16), vs)
            c = c + jnp.sum(lk, axis=-1, keepdims=True)
            out.append((c, acc))
        return tuple(out)

    init = tuple((jnp.zeros((tq, LANES), F32), jnp.zeros((tq, LANES), F32)) for _ in range(nh))
    carry = tile(i, init, True)
    carry = lax.fori_loop(0, i, lambda it, cr: tile(i - 1 - it, cr, False), carry)
    o = carry[0][1]
    for h in range(1, nh):
        o = jnp.where(lane // dh == h, carry[h][1], o)
    o_ref[...] = o.astype(BF16)


def _sb_attention(q, k, v, B, S, tq):
    T, W = q.shape
    nq = S // tq
    qspec = pl.BlockSpec((tq, LANES), lambda b, p, i: (b * nq + i, p))
    kvspec = pl.BlockSpec((S, LANES), lambda b, p, i: (b, p))
    return pl.pallas_call(
        functools.partial(_sb_kernel, tq=tq, dh=SB_HEAD_DIM),
        grid=(B, W // LANES, nq),
        in_specs=[qspec, kvspec, kvspec],
        out_specs=qspec,
        out_shape=jax.ShapeDtypeStruct((T, W), BF16),
        compiler_params=_params(("parallel", "parallel", "arbitrary")),
        name="sb_attn",
    )(q, k, v)


def _mlp(x, g_ref, w1_ref, w2_ref, ffc):
    h = _rms(x, g_ref[...]).astype(BF16)
    acc = x
    for c in range(w1_ref.shape[1] // ffc):
        a = jnp.maximum(_dot(h, w1_ref[:, c * ffc:(c + 1) * ffc]), 0.0)
        acc = acc + _dot((a * a).astype(BF16), w2_ref[c * ffc:(c + 1) * ffc, :])
    return acc


def _outmlp_kernel(x_ref, ylru_ref, ysb_ref, wo_ref, g_ref, w1_ref, w2_ref, o_ref, *, lru_w, ffc):
    o_ref[...] = x_ref[...] + _dot(ylru_ref[...], wo_ref[0:lru_w, :]) + _dot(ysb_ref[...], wo_ref[lru_w:, :])
    o_ref[...] = _mlp(o_ref[...], g_ref, w1_ref, w2_ref, ffc)


def _outmlp(x, ylru, ysb, wo, g, w1, w2, tm, ffc):
    T, D = x.shape
    row = lambda w_: pl.BlockSpec((tm, w_), lambda i: (i, 0))
    return pl.pallas_call(
        functools.partial(_outmlp_kernel, lru_w=ylru.shape[1], ffc=ffc),
        grid=(T // tm,),
        in_specs=[row(D), row(ylru.shape[1]), row(ysb.shape[1]), _const_spec(wo.shape),
                  _const_spec(g.shape), _const_spec(w1.shape), _const_spec(w2.shape)],
        out_specs=row(D),
        out_shape=jax.ShapeDtypeStruct((T, D), F32),
        compiler_params=_params(("parallel",)),
        name="outproj_mlp",
    )(x, ylru, ysb, wo, g, w1, w2)


def _glu_kernel(x_ref, g_ref, w_ref, b_ref, u_ref, *, cwid):
    h = _rms(x_ref[...], g_ref[...]).astype(BF16)
    ua = _dot(h, w_ref[:, 0:cwid]) + b_ref[:, 0:cwid]
    ub = _dot(h, w_ref[:, cwid:]) + b_ref[:, cwid:]
    u_ref[...] = ua * jax.nn.sigmoid(ub)


def _glu(x, g, w, b, tm):
    T, D = x.shape
    cwid = w.shape[1] // 2
    return pl.pallas_call(
        functools.partial(_glu_kernel, cwid=cwid),
        grid=(T // tm,),
        in_specs=[pl.BlockSpec((tm, D), lambda i: (i, 0)), _const_spec(g.shape), _const_spec(w.shape),
                  _const_spec(b.shape)],
        out_specs=pl.BlockSpec((tm, cwid), lambda i: (i, 0)),
        out_shape=jax.ShapeDtypeStruct((T, cwid), F32),
        compiler_params=_params(("parallel",)),
        name="pw1_glu",
    )(x, g, w, b)


def _convmlp_kernel(x_ref, u_ref, halo_ref, dw_ref, db_ref, lg_ref, lb_ref, p2_ref, p2b_ref,
                    g_ref, w1_ref, w2_ref, fg_ref, o_ref, upad, conv_s, *, tm, kw, halo, ns, rc, ffc):
    i = pl.program_id(0)
    first = (i % ns) == 0
    upad[0:halo, :] = jnp.where(first, 0.0, halo_ref[...])
    upad[halo:halo + tm, :] = u_ref[...]
    off0 = halo - (kw - 1)
    cwid = u_ref.shape[1]

    def chunk(ci, _):
        st = pl.multiple_of(ci * rc, rc)
        for lc in range(cwid // LANES):
            ls = slice(lc * LANES, (lc + 1) * LANES)
            win = upad[pl.ds(st, rc + halo), ls]
            shifted = [win] + [win[r:r + rc + halo - SUBLANES, :] for r in range(1, SUBLANES)]
            acc = jnp.broadcast_to(db_ref[:, ls], (rc, LANES))
            for k in range(kw):
                r, a0 = (off0 + k) % SUBLANES, (off0 + k) // SUBLANES * SUBLANES
                acc = acc + dw_ref[k:k + 1, ls] * shifted[r][a0:a0 + rc, :]
            conv_s[pl.ds(st, rc), ls] = acc
        return 0

    lax.fori_loop(0, tm // rc, chunk, 0)

    y = conv_s[...]
    mu = jnp.mean(y, axis=-1, keepdims=True)
    yc = y - mu
    var = jnp.mean(yc * yc, axis=-1, keepdims=True)
    yn = yc * lax.rsqrt(var + LN_EPS) * lg_ref[...] + lb_ref[...]
    sw = yn * jax.nn.sigmoid(yn)
    o_ref[...] = x_ref[...] + _dot(sw.astype(BF16), p2_ref[...]) + p2b_ref[...]
    x4 = _mlp(o_ref[...], g_ref, w1_ref, w2_ref, ffc)
    o_ref[...] = _rms(x4, fg_ref[...])


def _convmlp(x, u, dw, db, lg, lb, p2, p2b, g, w1, w2, fg, S, tm, ffc):
    T, D = x.shape
    cwid = u.shape[1]
    kw = dw.shape[0]
    halo = -(-(kw - 1) // SUBLANES) * SUBLANES
    ns = S // tm
    hb = tm // halo
    row = lambda w_: pl.BlockSpec((tm, w_), lambda i: (i, 0))
    halo_spec = pl.BlockSpec((halo, cwid), lambda i: (jnp.maximum(i * hb - 1, 0), 0))
    consts = [dw, db, lg, lb, p2, p2b, g, w1, w2, fg]
    return pl.pallas_call(
        functools.partial(_convmlp_kernel, tm=tm, kw=kw, halo=halo, ns=ns, rc=SUBLANES * 4, ffc=ffc),
        grid=(T // tm,),
        in_specs=[row(D), row(cwid), halo_spec] + [_const_spec(a.shape) for a in consts],
        out_specs=row(D),
        out_shape=jax.ShapeDtypeStruct((T, D), F32),
        scratch_shapes=[pltpu.VMEM((tm + halo, cwid), F32), pltpu.VMEM((tm, cwid), F32)],
        compiler_params=_params(("parallel",)),
        name="conv_mlp",
    )(x, u, u, *consts)


def _block_diag(w):
    H, d, _ = w.shape
    eye = jnp.eye(H, dtype=w.dtype)
    return (eye[:, None, :, None] * w[:, :, None, :]).reshape(H * d, H * d)


def kernel(x, ev_norm_g, ev_w_in, ev_conv_w, ev_conv_b, ev_gate_a_w, ev_gate_a_b, ev_gate_x_w, ev_gate_x_b, ev_lam, ev_w_out, od_norm_g, od_pw1_w, od_pw1_b, od_dw_w, od_dw_b, od_ln_g, od_ln_b, od_pw2_w, od_pw2_b, mlp_norm_g, mlp_w1, mlp_w2, final_g):
    B, S, D = x.shape
    T = B * S
    depth = mlp_w1.shape[0]
    lru_w = ev_conv_w.shape[-1]
    sb_w = (ev_w_in.shape[-1] - 2 * lru_w) // 3
    tm = min(512, S)
    tq = min(256, S)
    ffc = 1024
    r2 = lambda a: a.reshape(1, -1)

    xt = x.reshape(T, D)
    for layer in range(depth):
        j = layer // 2
        w1 = mlp_w1[layer].astype(BF16)
        w2 = mlp_w2[layer].astype(BF16)
        mg = r2(mlp_norm_g[layer])
        last = layer == depth - 1
        if layer % 2 == 0:
            gate, xrec, q, k, v = _inproj(xt, r2(ev_norm_g[j]), ev_w_in[j].astype(BF16), lru_w, sb_w, tm)
            wg = jnp.concatenate([_block_diag(ev_gate_a_w[j]), _block_diag(ev_gate_x_w[j])], axis=1).astype(BF16)
            bg = jnp.concatenate([ev_gate_a_b[j], ev_gate_x_b[j]]).reshape(1, -1)
            c = (LRU_C * jax.nn.log_sigmoid(ev_lam[j].astype(F32))).reshape(1, -1)
            ylru = _lru(xrec, gate, ev_conv_w[j], r2(ev_conv_b[j]), wg, bg, c, B, S, tm)
            ysb = _sb_attention(q, k, v, B, S, tq)
            xt = _outmlp(xt, ylru, ysb, ev_w_out[j].astype(BF16), mg, w1, w2, tm, ffc)
            if last:
                raise NotImplementedError("final norm is fused into the odd-layer kernel only")
        else:
            u = _glu(xt, r2(od_norm_g[j]), od_pw1_w[j].astype(BF16), r2(od_pw1_b[j]), tm)
            fg = r2(final_g) if last else None
            assert last, "final norm is fused into the last (odd) layer"
            xt = _convmlp(xt, u, od_dw_w[j], r2(od_dw_b[j]), r2(od_ln_g[j]), r2(od_ln_b[j]),
                          od_pw2_w[j].astype(BF16), r2(od_pw2_b[j]), mg, w1, w2, fg, S, tm, ffc)
    return xt.reshape(B, S, D)
```
